```python
import jax, jax.numpy as jnp
from jax import lax
import numpy as np

D_MODEL = 2048
BATCH = 2
SEQ = 4096
DEPTH = 4
DEC_BATCH = 128
DEC_SEQ = 1
PAST_LEN = 8192
PAGE_SIZE = 128

N_EVEN = (DEPTH + 1) // 2
N_ODD = DEPTH // 2
MLA_HEADS = 8
MLA_NOPE = 128
MLA_ROPE = 64
MLA_V = 128
Q_LORA = 512
KV_LORA = 256
ROPE_THETA = 10000.0
MLA_SCALE = (MLA_NOPE + MLA_ROPE) ** -0.5
SB_HEADS = 8
SB_KV_HEADS = 2
SB_DIM = 128
SB_SCALE = SB_DIM ** -0.5
NSA_HEADS = 16
NSA_DIM = 128
NSA_BLOCK = 64
NSA_TOPK = 16
NSA_WINDOW = 512
NSA_SCALE = NSA_DIM ** -0.5
NSA_FORCE = NSA_HEADS + 1.0
N_GROUPS = 4
EXPERTS_PER_GROUP = 4
N_EXPERTS = N_GROUPS * EXPERTS_PER_GROUP
EXPERT_TOPK = 2
EXPERT_HIDDEN = 512

Q_BLOCK = 128
NORM_EPS = 1e-6
NEG_INF = -1e30

EVEN_SPLITS = (Q_LORA, Q_LORA + KV_LORA, Q_LORA + KV_LORA + MLA_ROPE,
               Q_LORA + KV_LORA + MLA_ROPE + SB_HEADS * SB_DIM,
               Q_LORA + KV_LORA + MLA_ROPE + SB_HEADS * SB_DIM + SB_KV_HEADS * SB_DIM)
EVEN_IN = EVEN_SPLITS[-1] + SB_KV_HEADS * SB_DIM
EVEN_MIX = MLA_HEADS * MLA_V + SB_HEADS * SB_DIM
ODD_SPLITS = tuple(NSA_HEADS * NSA_DIM + i * NSA_DIM for i in range(7))
ODD_IN = ODD_SPLITS[-1] + 3 * NSA_HEADS
ODD_MIX = NSA_HEADS * NSA_DIM

kernel_name = 'hybrid_mla_stickbreak_nsa_hmoe_step'


def rmsnorm(x, g=None):
    xf = x.astype(jnp.float32)
    y = (xf * lax.rsqrt(jnp.mean(xf * xf, axis=-1, keepdims=True) + NORM_EPS)).astype(x.dtype)
    return y if g is None else y * g


def adaln(c, w, b):
    m = jax.nn.silu(c) @ w + b
    return [a[:, None, :] for a in jnp.split(m, 6, axis=-1)]


def modulate(x, shift, scale):
    return rmsnorm(x) * (1.0 + scale) + shift


def rope(x, pos):
    half = x.shape[-1] // 2
    inv_freq = ROPE_THETA ** (-jnp.arange(half, dtype=jnp.float32) / half)
    ang = pos.astype(jnp.float32)[:, None] * inv_freq[None, :]
    shp = (ang.shape[0],) + (1,) * (x.ndim - 3) + (half,)
    cos, sin = jnp.cos(ang).reshape(shp), jnp.sin(ang).reshape(shp)
    xf = x.astype(jnp.float32)
    x1, x2 = xf[..., :half], xf[..., half:]
    return jnp.concatenate([x1 * cos - x2 * sin, x2 * cos + x1 * sin], axis=-1).astype(x.dtype)


def alibi_slopes(n):
    return jnp.exp2(-8.0 * jnp.arange(1, n + 1, dtype=jnp.float32) / n)


def masked_softmax(s, mask, axis):
    s = jnp.where(mask, s, NEG_INF)
    m = jnp.max(s, axis=axis, keepdims=True)
    p = jnp.where(mask, jnp.exp(s - m), 0.0)
    return p / jnp.maximum(jnp.sum(p, axis=axis, keepdims=True), 1e-30)


def gather_pages(pool, layer, page_table):
    g = pool[layer, page_table]
    return g.reshape((g.shape[0], g.shape[1] * g.shape[2]) + g.shape[3:])


def over_query_blocks(fn, n_q):
    out = lax.map(fn, jnp.arange(0, n_q, Q_BLOCK))
    return jnp.swapaxes(out, 0, 1).reshape((out.shape[1], out.shape[0] * Q_BLOCK) + out.shape[3:])


def mla_attend(q_lat, q_pe, lat, kpe, qpos, kpos):
    s = (jnp.einsum('bqhc,bkc->bhqk', q_lat, lat) + jnp.einsum('bqhr,bkr->bhqk', q_pe, kpe)).astype(jnp.float32) * MLA_SCALE
    p = masked_softmax(s, kpos[None, :] <= qpos[:, None], -1)
    return jnp.einsum('bhqk,bkc->bqhc', p.astype(lat.dtype), lat)


def sb_attend(q, k, v, qpos, kpos):
    B, Q = q.shape[0], q.shape[1]
    qg = q.reshape(B, Q, SB_KV_HEADS, SB_HEADS // SB_KV_HEADS, SB_DIM)
    z = jnp.einsum('bqngd,bknd->bngqk', qg, k).astype(jnp.float32) * SB_SCALE
    mask = kpos[None, :] < qpos[:, None]
    log_rem = jnp.where(mask, jax.nn.log_sigmoid(-z), 0.0)
    between = lax.cumsum(log_rem, axis=4, reverse=True) - log_rem
    a = jnp.where(mask, jnp.exp(jax.nn.log_sigmoid(z) + between), 0.0)
    o = jnp.einsum('bngqk,bknd->bqngd', a.astype(v.dtype), v)
    return o.reshape(B, Q, SB_HEADS * SB_DIM)


def even_attend(q_lat, q_pe, q_sb, qpos, lat, kpe, k_sb, v_sb, kpos, w_kv_up):
    B, Q = q_lat.shape[0], q_lat.shape[1]
    o_lat = mla_attend(q_lat, q_pe, lat, kpe, qpos, kpos)
    o_mla = jnp.einsum('bqhc,chv->bqhv', o_lat, w_kv_up[..., MLA_NOPE:]).reshape(B, Q, MLA_HEADS * MLA_V)
    return jnp.concatenate([o_mla, sb_attend(q_sb, k_sb, v_sb, qpos, kpos)], axis=-1)


def even_mixer(h, pos, past, w_in, g_qn, w_q_up, g_kvn, w_kv_up, w_out):
    B, T, _ = h.shape
    c_q, c_kv, k_pe, q_sb, k_sb, v_sb = jnp.split(h @ w_in, EVEN_SPLITS, axis=-1)
    qh = jnp.einsum('btc,chr->bthr', rmsnorm(c_q, g_qn), w_q_up)
    q_pe = rope(qh[..., MLA_NOPE:], pos)
    q_lat = jnp.einsum('bthn,chn->bthc', qh[..., :MLA_NOPE], w_kv_up[..., :MLA_NOPE])
    lat = rmsnorm(c_kv, g_kvn)
    k_pe = rope(k_pe, pos)
    q_sb = q_sb.reshape(B, T, SB_HEADS, SB_DIM)
    new = (lat, k_pe, k_sb.reshape(B, T, SB_KV_HEADS, SB_DIM), v_sb.reshape(B, T, SB_KV_HEADS, SB_DIM))
    if past is None:
        def blk(s):
            sl = lambda a: lax.dynamic_slice_in_dim(a, s, Q_BLOCK, axis=1)
            return even_attend(sl(q_lat), sl(q_pe), sl(q_sb), s + jnp.arange(Q_BLOCK), *new, pos, w_kv_up)
        o = over_query_blocks(blk, T)
    else:
        keys = tuple(jnp.concatenate([p, n], axis=1) for p, n in zip(past, new))
        o = even_attend(q_lat, q_pe, q_sb, pos, *keys, jnp.arange(keys[0].shape[1]), w_kv_up)
    return o @ w_out, new


def nsa_key_blocks(ck, cv, sk, sv):
    K = ck.shape[1]
    nb = -(-K // NSA_BLOCK)
    pad = nb * NSA_BLOCK - K
    blocks = lambda a: jnp.pad(a, ((0, 0), (0, pad), (0, 0))).reshape(a.shape[0], nb, NSA_BLOCK, a.shape[-1])
    return jnp.mean(blocks(ck), axis=2), jnp.mean(blocks(cv), axis=2), blocks(sk), blocks(sv)


def nsa_attend(q, gates, qpos, ck, cv, skb, svb, wk, wv, wpos):
    nb = ck.shape[1]
    slopes = alibi_slopes(NSA_HEADS)
    blk = jnp.arange(nb)
    cend = (blk + 1) * NSA_BLOCK - 1
    cdist = (qpos[:, None] - cend[None, :]).astype(jnp.float32)
    s = jnp.einsum('bqhd,bnd->bqhn', q, ck).astype(jnp.float32) * NSA_SCALE - slopes[:, None] * cdist[:, None, :]
    p_cmp = masked_softmax(s, (cdist >= 0)[:, None, :], -1)
    o_cmp = jnp.einsum('bqhn,bnd->bqhd', p_cmp.astype(cv.dtype), cv)
    qblk = qpos // NSA_BLOCK
    forced = (blk[None, :] == 0) | (blk[None, :] == qblk[:, None]) | (blk[None, :] == qblk[:, None] - 1)
    imp = jnp.where(forced, NSA_FORCE, jnp.sum(p_cmp, axis=2))
    imp = jnp.where(blk[None, :] <= qblk[:, None], imp, -1.0)
    top_v, top_i = lax.top_k(imp, min(NSA_TOPK, nb))
    gk = jax.vmap(lambda kb, i: kb[i])(skb, top_i)
    gv = jax.vmap(lambda vb, i: vb[i])(svb, top_i)
    spos = top_i[..., None] * NSA_BLOCK + jnp.arange(NSA_BLOCK)
    sdist = (qpos[None, :, None, None] - spos).astype(jnp.float32)
    smask = (sdist >= 0) & (top_v >= 0)[..., None]
    s = jnp.einsum('bqhd,bqkld->bqhkl', q, gk).astype(jnp.float32) * NSA_SCALE - slopes[:, None, None] * sdist[:, :, None]
    p_sel = masked_softmax(s, smask[:, :, None], (-2, -1))
    o_sel = jnp.einsum('bqhkl,bqkld->bqhd', p_sel.astype(gv.dtype), gv)
    wdist = qpos[:, None] - wpos[None, :]
    wmask = (wdist >= 0) & (wdist < NSA_WINDOW) & (wpos >= 0)[None, :]
    s = jnp.einsum('bqhd,bkd->bqhk', q, wk).astype(jnp.float32) * NSA_SCALE - slopes[:, None] * wdist.astype(jnp.float32)[:, None, :]
    p_win = masked_softmax(s, wmask[:, None, :], -1)
    o_win = jnp.einsum('bqhk,bkd->bqhd', p_win.astype(wv.dtype), wv)
    return gates[..., 0:1] * o_cmp + gates[..., 1:2] * o_sel + gates[..., 2:3] * o_win


def odd_mixer(h, pos, past, win_buf, w_in, w_out):
    B, T, _ = h.shape
    q, ck, cv, sk, sv, wk, wv, g = jnp.split(h @ w_in, ODD_SPLITS, axis=-1)
    q = q.reshape(B, T, NSA_HEADS, NSA_DIM)
    gates = jax.nn.sigmoid(g.astype(jnp.float32)).astype(h.dtype).reshape(B, T, NSA_HEADS, 3)
    new = (ck, cv, sk, sv)
    if past is None:
        kb = nsa_key_blocks(*new)
        band = Q_BLOCK + NSA_WINDOW
        pad = ((0, 0), (NSA_WINDOW, 0), (0, 0))
        wk_pad, wv_pad = jnp.pad(wk, pad), jnp.pad(wv, pad)
        def blk(s):
            sl = lambda a: lax.dynamic_slice_in_dim(a, s, Q_BLOCK, axis=1)
            bd = lambda a: lax.dynamic_slice_in_dim(a, s, band, axis=1)
            wpos = s - NSA_WINDOW + jnp.arange(band)
            return nsa_attend(sl(q), sl(gates), s + jnp.arange(Q_BLOCK), *kb, bd(wk_pad), bd(wv_pad), wpos)
        o = over_query_blocks(blk, T)
        keep = min(NSA_WINDOW, T)
        win_new = (wk[:, T - keep:], wv[:, T - keep:])
    else:
        full = tuple(jnp.concatenate([p, n], axis=1) for p, n in zip(past, new))
        kb = nsa_key_blocks(*full)
        n_buf = win_buf[0].shape[1]
        wk_all = jnp.concatenate([win_buf[0], wk], axis=1)
        wv_all = jnp.concatenate([win_buf[1], wv], axis=1)
        wpos = past[0].shape[1] - n_buf + jnp.arange(n_buf + T)
        o = nsa_attend(q, gates, pos, *kb, wk_all, wv_all, wpos)
        win_new = (wk_all[:, T:], wv_all[:, T:])
    return o.reshape(B, T, ODD_MIX) @ w_out, new, win_new


def hmoe(h, w_group, b_group, w_router, b_router, w_gate, w_up, w_down):
    B, T, _ = h.shape
    g_logits = (h @ w_group + b_group).astype(jnp.float32)
    e_logits = (h @ w_router + b_router).astype(jnp.float32).reshape(B, T, N_GROUPS, EXPERTS_PER_GROUP)
    g_val, g_idx = lax.top_k(g_logits, 1)
    p_group = jnp.exp(g_val[..., 0] - jax.nn.logsumexp(g_logits, axis=-1))
    within = jnp.take_along_axis(e_logits, g_idx[..., None], axis=2)[:, :, 0]
    e_val, e_idx = lax.top_k(within, EXPERT_TOPK)
    w = jax.nn.softmax(e_val, axis=-1) * p_group[..., None]
    eid = g_idx * EXPERTS_PER_GROUP + e_idx
    gate = jnp.einsum('btk,btke->bte', w, jax.nn.one_hot(eid, N_EXPERTS, dtype=jnp.float32)).astype(h.dtype)
    hid = jax.nn.silu(jnp.einsum('btd,edf->btef', h, w_gate)) * jnp.einsum('btd,edf->btef', h, w_up)
    return jnp.einsum('btef,efd->btd', hid * gate[..., None], w_down)


def setup_inputs(seed: int = 0) -> dict:
    key = jax.random.key(seed)
    counter = [0]

    def nrm(shape, scale=None):
        counter[0] += 1
        a = jax.random.normal(jax.random.fold_in(key, counter[0]), shape, jnp.float32)
        return a if scale is None else a * scale

    n_pages = PAST_LEN // PAGE_SIZE
    n_used = DEC_BATCH * n_pages
    pool = n_used + n_used // 4
    n_buf = min(NSA_WINDOW, PAST_LEN)
    counter[0] += 1
    perm = jax.random.permutation(jax.random.fold_in(key, counter[0]), pool)
    page_table = perm[:n_used].reshape(DEC_BATCH, n_pages).astype(jnp.int32)
    D = D_MODEL
    return {
        'x_prompt': nrm((BATCH, SEQ, D)),
        'x_sample': nrm((DEC_BATCH, DEC_SEQ, D)),
        'cache_mla_latent': nrm((N_EVEN, pool, PAGE_SIZE, KV_LORA)),
        'cache_mla_krope': nrm((N_EVEN, pool, PAGE_SIZE, MLA_ROPE)),
        'cache_sb_k': nrm((N_EVEN, pool, PAGE_SIZE, SB_KV_HEADS, SB_DIM)),
        'cache_sb_v': nrm((N_EVEN, pool, PAGE_SIZE, SB_KV_HEADS, SB_DIM)),
        'cache_nsa_cmp_k': nrm((N_ODD, pool, PAGE_SIZE, NSA_DIM)),
        'cache_nsa_cmp_v': nrm((N_ODD, pool, PAGE_SIZE, NSA_DIM)),
        'cache_nsa_sel_k': nrm((N_ODD, pool, PAGE_SIZE, NSA_DIM)),
        'cache_nsa_sel_v': nrm((N_ODD, pool, PAGE_SIZE, NSA_DIM)),
        'state_nsa_win_k': nrm((N_ODD, DEC_BATCH, n_buf, NSA_DIM)),
        'state_nsa_win_v': nrm((N_ODD, DEC_BATCH, n_buf, NSA_DIM)),
        'page_table': page_table,
        'c_prompt': nrm((BATCH, D)),
        'c_sample': nrm((DEC_BATCH, D)),
        'w_mod': nrm((DEPTH, D, 6 * D), 0.5 * D ** -0.5),
        'b_mod': nrm((DEPTH, 6 * D), 0.01),
        'w_even_in': nrm((N_EVEN, D, EVEN_IN), D ** -0.5),
        'g_q_norm': 1.0 + nrm((N_EVEN, Q_LORA), 0.02),
        'w_q_up': nrm((N_EVEN, Q_LORA, MLA_HEADS, MLA_NOPE + MLA_ROPE), Q_LORA ** -0.5),
        'g_kv_norm': 1.0 + nrm((N_EVEN, KV_LORA), 0.02),
        'w_kv_up': nrm((N_EVEN, KV_LORA, MLA_HEADS, MLA_NOPE + MLA_V), KV_LORA ** -0.5),
        'w_even_out': nrm((N_EVEN, EVEN_MIX, D), EVEN_MIX ** -0.5),
        'w_odd_in': nrm((N_ODD, D, ODD_IN), D ** -0.5),
        'w_odd_out': nrm((N_ODD, ODD_MIX, D), ODD_MIX ** -0.5),
        'w_group': nrm((DEPTH, D, N_GROUPS), D ** -0.5),
        'b_group': nrm((DEPTH, N_GROUPS), 0.01),
        'w_router': nrm((DEPTH, D, N_EXPERTS), D ** -0.5),
        'b_router': nrm((DEPTH, N_EXPERTS), 0.01),
        'w_exp_gate': nrm((DEPTH, N_EXPERTS, D, EXPERT_HIDDEN), D ** -0.5),
        'w_exp_up': nrm((DEPTH, N_EXPERTS, D, EXPERT_HIDDEN), D ** -0.5),
        'w_exp_down': nrm((DEPTH, N_EXPERTS, EXPERT_HIDDEN, D), EXPERT_HIDDEN ** -0.5),
        'g_final': 1.0 + nrm((D,), 0.02),
    }


def reference(x_prompt, x_sample, cache_mla_latent, cache_mla_krope, cache_sb_k, cache_sb_v,
              cache_nsa_cmp_k, cache_nsa_cmp_v, cache_nsa_sel_k, cache_nsa_sel_v,
              state_nsa_win_k, state_nsa_win_v, page_table, c_prompt, c_sample,
              w_mod, b_mod, w_even_in, g_q_norm, w_q_up, g_kv_norm, w_kv_up, w_even_out,
              w_odd_in, w_odd_out, w_group, b_group, w_router, b_router,
              w_exp_gate, w_exp_up, w_exp_down, g_final):
    past_len = page_table.shape[1] * cache_mla_latent.shape[2]
    pos_p = jnp.arange(x_prompt.shape[1])
    pos_s = past_len + jnp.arange(x_sample.shape[1])
    xp, xs = x_prompt, x_sample
    even_p, even_s = ([], [], [], []), ([], [], [], [])
    odd_p, odd_s = ([], [], [], [], [], []), ([], [], [], [], [], [])
    for l in range(DEPTH):
        mp = adaln(c_prompt, w_mod[l], b_mod[l])
        ms = adaln(c_sample, w_mod[l], b_mod[l])
        hp = modulate(xp, mp[0], mp[1])
        hs = modulate(xs, ms[0], ms[1])
        i = l // 2
        if l % 2 == 0:
            wts = (w_even_in[i], g_q_norm[i], w_q_up[i], g_kv_norm[i], w_kv_up[i], w_even_out[i])
            yp, new_p = even_mixer(hp, pos_p, None, *wts)
            past = tuple(gather_pages(c, i, page_table) for c in (cache_mla_latent, cache_mla_krope, cache_sb_k, cache_sb_v))
            ys, new_s = even_mixer(hs, pos_s, past, *wts)
            for lst, a in zip(even_p, new_p):
                lst.append(a)
            for lst, a in zip(even_s, new_s):
                lst.append(a)
        else:
            yp, new_p, win_p = odd_mixer(hp, pos_p, None, None, w_odd_in[i], w_odd_out[i])
            past = tuple(gather_pages(c, i, page_table) for c in (cache_nsa_cmp_k, cache_nsa_cmp_v, cache_nsa_sel_k, cache_nsa_sel_v))
            ys, new_s, win_s = odd_mixer(hs, pos_s, past, (state_nsa_win_k[i], state_nsa_win_v[i]), w_odd_in[i], w_odd_out[i])
            for lst, a in zip(odd_p, new_p + win_p):
                lst.append(a)
            for lst, a in zip(odd_s, new_s + win_s):
                lst.append(a)
        xp = xp + mp[2] * yp
        xs = xs + ms[2] * ys
        moe_w = (w_group[l], b_group[l], w_router[l], b_router[l], w_exp_gate[l], w_exp_up[l], w_exp_down[l])
        xp = xp + mp[5] * hmoe(modulate(xp, mp[3], mp[4]), *moe_w)
        xs = xs + ms[5] * hmoe(modulate(xs, ms[3], ms[4]), *moe_w)
    y_prompt = rmsnorm(xp, g_final)
    y_sample = rmsnorm(xs, g_final)
    lat_p, kr_p, sbk_p, sbv_p = [jnp.stack(a) for a in even_p]
    lat_s, kr_s, sbk_s, sbv_s = [jnp.stack(a) for a in even_s]
    ck_p, cv_p, sk_p, sv_p, wk_p, wv_p = [jnp.stack(a) for a in odd_p]
    ck_s, cv_s, sk_s, sv_s, wk_s, wv_s = [jnp.stack(a) for a in odd_s]
    return (y_prompt, y_sample, lat_p, lat_s, kr_p, kr_s, sbk_p, sbk_s, sbv_p, sbv_s,
            ck_p, ck_s, cv_p, cv_s, sk_p, sk_s, sv_p, sv_s, wk_p, wk_s, wv_p, wv_s)
```

```python
import functools

import jax
import jax.numpy as jnp
from jax import lax
from jax.experimental import pallas as pl
from jax.experimental.pallas import tpu as pltpu

MLA_HEADS = 8
MLA_NOPE = 128
MLA_ROPE = 64
MLA_V = 128
Q_LORA = 512
KV_LORA = 256
ROPE_THETA = 10000.0
MLA_SCALE = (MLA_NOPE + MLA_ROPE) ** -0.5
SB_HEADS = 8
SB_KV_HEADS = 2
SB_DIM = 128
SB_SCALE = SB_DIM ** -0.5
NSA_HEADS = 16
NSA_DIM = 128
NSA_BLOCK = 64
NSA_TOPK = 16
NSA_WINDOW = 512
NSA_SCALE = NSA_DIM ** -0.5
NSA_FORCE = NSA_HEADS + 1.0
N_GROUPS = 4
EXPERTS_PER_GROUP = 4
N_EXPERTS = 16
EXPERT_HIDDEN = 512
NORM_EPS = 1e-6
NEG_INF = -1e30

LANES = 128
VMEM_LIMIT = 56 * 1024 * 1024
Q_KEY_WIDTH = KV_LORA + LANES
EVEN_COLS = 2560
ODD_COLS = 2944
GATE_LANE0 = N_GROUPS

_F32 = jnp.float32
_BF16 = jnp.bfloat16


def _bf(x):
    return x.astype(_BF16)


def _dot(a, b):
    return jnp.dot(a, b, preferred_element_type=_F32)


def _dot_nt(a, b):
    return lax.dot_general(a, b, (((1,), (1,)), ((), ())), preferred_element_type=_F32)


def _sigmoid(x):
    return 1.0 / (1.0 + jnp.exp(-x))


def _log_sigmoid(x):
    return jnp.minimum(x, 0.0) - jnp.log1p(jnp.exp(-jnp.abs(x)))


def _modnorm(x, shift, scale):
    ms = jnp.mean(x * x, axis=-1, keepdims=True)
    return (x * lax.rsqrt(ms + NORM_EPS)) * (1.0 + scale) + shift


def _params(sem):
    return pltpu.CompilerParams(dimension_semantics=sem, vmem_limit_bytes=VMEM_LIMIT)


def _mod_spec(mod, tm):
    d = mod.shape[-1]
    if mod.shape[1] == 1:
        return pl.BlockSpec((1, 1, d), lambda b, i: (b, 0, 0))
    return pl.BlockSpec((1, tm, d), lambda b, i: (b, i, 0))


def _full_spec(a):
    nd = a.ndim
    return pl.BlockSpec(a.shape, lambda *_: (0,) * nd)


def _adaln_kernel(c_ref, w_ref, b_ref, o_ref):
    c = c_ref[...]
    o_ref[...] = _dot(_bf(c * _sigmoid(c)), _bf(w_ref[...])) + b_ref[...]


def _adaln(c_all, w_mod, b_mod):
    depth, d, n = w_mod.shape
    mc = c_all.shape[0]
    tn = 1024
    return pl.pallas_call(
        _adaln_kernel,
        grid=(depth, n // tn),
        in_specs=[pl.BlockSpec((mc, d), lambda l, j: (0, 0)),
                  pl.BlockSpec((None, d, tn), lambda l, j: (l, 0, j)),
                  pl.BlockSpec((None, 1, tn), lambda l, j: (l, 0, j))],
        out_specs=pl.BlockSpec((None, mc, tn), lambda l, j: (l, 0, j)),
        out_shape=jax.ShapeDtypeStruct((depth, mc, n), _F32),
        compiler_params=_params(("arbitrary", "arbitrary")),
        name="adaln",
    )(c_all, w_mod, b_mod.reshape(depth, 1, n))


def _even_in_kernel(x_ref, sh_ref, sc_ref, cos_ref, sin_ref, w_ref, gq_ref, gkv_ref, wq_ref, wk_ref,
                    lat_ref, kpe_ref, ksb_ref, vsb_ref, kfull_ref, ksbb_ref, vsbb_ref, q_ref, qsb_ref):
    h = _modnorm(x_ref[0], sh_ref[0], sc_ref[0])
    y = _dot(_bf(h), w_ref[...])
    cos = cos_ref[...]
    sin = sin_ref[...]
    c_q = y[:, 0:512]
    c_kv = y[:, 512:768]
    lat = c_kv * lax.rsqrt(jnp.mean(c_kv * c_kv, axis=-1, keepdims=True) + NORM_EPS) * gkv_ref[...]
    kpe = y[:, 2304:2432] * cos + y[:, 2432:2560] * sin
    ksb = y[:, 1792:2048]
    vsb = y[:, 2048:2304]
    lat_ref[0] = lat
    kpe_ref[0] = kpe[:, 0:MLA_ROPE]
    ksb_ref[0] = ksb
    vsb_ref[0] = vsb
    kfull_ref[0, :, 0:KV_LORA] = _bf(lat)
    kfull_ref[0, :, KV_LORA:Q_KEY_WIDTH] = _bf(kpe)
    ksbb_ref[0] = _bf(ksb)
    vsbb_ref[0] = _bf(vsb)
    cqn = c_q * lax.rsqrt(jnp.mean(c_q * c_q, axis=-1, keepdims=True) + NORM_EPS) * gq_ref[...]
    qh = _dot(_bf(cqn), wq_ref[...])
    for hd in range(MLA_HEADS):
        nope = qh[:, hd * 128:(hd + 1) * 128]
        q_lat = _dot(_bf(nope), wk_ref[hd])
        q_pe = (qh[:, 1024 + hd * 128:1024 + (hd + 1) * 128] * cos
                + qh[:, 2048 + hd * 128:2048 + (hd + 1) * 128] * sin)
        q_ref[0, hd, :, 0:KV_LORA] = _bf(q_lat * MLA_SCALE)
        q_ref[0, hd, :, KV_LORA:Q_KEY_WIDTH] = _bf(q_pe * MLA_SCALE)
        qsb_ref[0, hd] = _bf(y[:, 768 + hd * 128:768 + (hd + 1) * 128] * SB_SCALE)


def _even_in(x, shift, scale, cos, sin, w_ext, gq, gkv, wq_all, wk_t, tm):
    b, t, d = x.shape
    row = lambda n: pl.BlockSpec((1, tm, n), lambda bi, i: (bi, i, 0))
    head = lambda n: pl.BlockSpec((1, MLA_HEADS, tm, n), lambda bi, i: (bi, 0, i, 0))
    sds = jax.ShapeDtypeStruct
    return pl.pallas_call(
        _even_in_kernel,
        grid=(b, t // tm),
        in_specs=[row(d), _mod_spec(shift, tm), _mod_spec(scale, tm),
                  pl.BlockSpec((tm, LANES), lambda bi, i: (i, 0)),
                  pl.BlockSpec((tm, LANES), lambda bi, i: (i, 0)),
                  _full_spec(w_ext), _full_spec(gq), _full_spec(gkv), _full_spec(wq_all), _full_spec(wk_t)],
        out_specs=[row(KV_LORA), row(MLA_ROPE), row(256), row(256),
                   row(Q_KEY_WIDTH), row(256), row(256), head(Q_KEY_WIDTH), head(SB_DIM)],
        out_shape=[sds((b, t, KV_LORA), _F32), sds((b, t, MLA_ROPE), _F32),
                   sds((b, t, 256), _F32), sds((b, t, 256), _F32),
                   sds((b, t, Q_KEY_WIDTH), _BF16), sds((b, t, 256), _BF16), sds((b, t, 256), _BF16),
                   sds((b, MLA_HEADS, t, Q_KEY_WIDTH), _BF16), sds((b, SB_HEADS, t, SB_DIM), _BF16)],
        compiler_params=_params(("arbitrary", "arbitrary")),
        name="even_in",
    )(x, shift, scale, cos, sin, w_ext, gq, gkv, wq_all, wk_t)


def _mla_prompt_kernel(q_ref, k_ref, wv_ref, o_ref, m_ref, l_ref, acc_ref, *, tq, tk):
    i = pl.program_id(1)
    rows = MLA_HEADS * tq
    q = q_ref[0].reshape(rows, Q_KEY_WIDTH)
    m_ref[...] = jnp.full((rows, 1), NEG_INF, _F32)
    l_ref[...] = jnp.zeros((rows, 1), _F32)
    acc_ref[...] = jnp.zeros((rows, KV_LORA), _F32)
    qpos = i * tq + (lax.broadcasted_iota(jnp.int32, (rows, tk), 0) & (tq - 1))
    lane = lax.broadcasted_iota(jnp.int32, (rows, tk), 1)

    def body(c, carry):
        k = k_ref[0, pl.ds(pl.multiple_of(c * tk, tk), tk), :]
        s = _dot_nt(q, k)
        s = jnp.where(c * tk + lane <= qpos, s, NEG_INF)
        m_old = m_ref[...]
        m_new = jnp.maximum(m_old, jnp.max(s, axis=1, keepdims=True))
        alpha = jnp.exp(m_old - m_new)
        p = jnp.exp(s - m_new)
        l_ref[...] = alpha * l_ref[...] + jnp.sum(p, axis=1, keepdims=True)
        acc_ref[...] = alpha * acc_ref[...] + _dot(_bf(p), k[:, 0:KV_LORA])
        m_ref[...] = m_new
        return carry

    lax.fori_loop(0, (i * tq + tq + tk - 1) // tk, body, 0)
    o_lat = acc_ref[...] / jnp.maximum(l_ref[...], 1e-30)
    for hd in range(MLA_HEADS):
        o_ref[0, :, hd * MLA_V:(hd + 1) * MLA_V] = _bf(_dot(_bf(o_lat[hd * tq:(hd + 1) * tq]), wv_ref[hd]))


def _mla_prompt(q_full, k_full, wv):
    b, _, t, _ = q_full.shape
    tq = 128
    tk = min(256, t)
    rows = MLA_HEADS * tq
    return pl.pallas_call(
        functools.partial(_mla_prompt_kernel, tq=tq, tk=tk),
        grid=(b, t // tq),
        in_specs=[pl.BlockSpec((1, MLA_HEADS, tq, Q_KEY_WIDTH), lambda bi, i: (bi, 0, i, 0)),
                  pl.BlockSpec((1, t, Q_KEY_WIDTH), lambda bi, i: (bi, 0, 0)),
                  _full_spec(wv)],
        out_specs=pl.BlockSpec((1, tq, MLA_HEADS * MLA_V), lambda bi, i: (bi, i, 0)),
        out_shape=jax.ShapeDtypeStruct((b, t, MLA_HEADS * MLA_V), _BF16),
        scratch_shapes=[pltpu.VMEM((rows, 1), _F32), pltpu.VMEM((rows, 1), _F32),
                        pltpu.VMEM((rows, KV_LORA), _F32)],
        compiler_params=_params(("arbitrary", "arbitrary")),
        name="mla_prompt",
    )(q_full, k_full, wv)


def _suffix_matrix(n):
    r = lax.broadcasted_iota(jnp.int32, (n, n), 0)
    c = lax.broadcasted_iota(jnp.int32, (n, n), 1)
    return jnp.where(r > c, 1.0, 0.0).astype(_BF16)


def _split_dot(x, u):
    hi = _bf(x)
    lo = _bf(x - hi.astype(_F32))
    return _dot(hi, u) + _dot(lo, u)


def _sb_prompt_kernel(q_ref, k_ref, v_ref, o_ref, carry_ref, acc_ref, *, tq):
    i = pl.program_id(2)
    group = SB_HEADS // SB_KV_HEADS
    rows = group * tq
    q = q_ref[0].reshape(rows, SB_DIM)
    carry_ref[...] = jnp.zeros((rows, 1), _F32)
    acc_ref[...] = jnp.zeros((rows, SB_DIM), _F32)
    u = _suffix_matrix(tq)
    qpos = i * tq + (lax.broadcasted_iota(jnp.int32, (rows, tq), 0) & (tq - 1))
    lane = lax.broadcasted_iota(jnp.int32, (rows, tq), 1)

    def body(j, carry):
        c = i - j
        start = pl.multiple_of(c * tq, tq)
        k = k_ref[0, pl.ds(start, tq), :]
        v = v_ref[0, pl.ds(start, tq), :]
        z = _dot_nt(q, k)
        mask = c * tq + lane < qpos
        ls = _log_sigmoid(z)
        lr = jnp.where(mask, ls - z, 0.0)
        between = _split_dot(lr, u) + carry_ref[...]
        a = jnp.where(mask, jnp.exp(ls + between), 0.0)
        acc_ref[...] += _dot(_bf(a), v)
        carry_ref[...] += jnp.sum(lr, axis=1, keepdims=True)
        return carry

    lax.fori_loop(0, i + 1, body, 0)
    acc = acc_ref[...]
    for hd in range(group):
        o_ref[0, :, hd * SB_DIM:(hd + 1) * SB_DIM] = _bf(acc[hd * tq:(hd + 1) * tq])


def _sb_prompt(q_sb, k_bf, v_bf):
    b, _, t, _ = q_sb.shape
    tq = 128
    group = SB_HEADS // SB_KV_HEADS
    rows = group * tq
    return pl.pallas_call(
        functools.partial(_sb_prompt_kernel, tq=tq),
        grid=(b, SB_KV_HEADS, t // tq),
        in_specs=[pl.BlockSpec((1, group, tq, SB_DIM), lambda bi, g, i: (bi, g, i, 0)),
                  pl.BlockSpec((1, t, SB_DIM), lambda bi, g, i: (bi, 0, g)),
                  pl.BlockSpec((1, t, SB_DIM), lambda bi, g, i: (bi, 0, g))],
        out_specs=pl.BlockSpec((1, tq, group * SB_DIM), lambda bi, g, i: (bi, i, g)),
        out_shape=jax.ShapeDtypeStruct((b, t, SB_HEADS * SB_DIM), _BF16),
        scratch_shapes=[pltpu.VMEM((rows, 1), _F32), pltpu.VMEM((rows, SB_DIM), _F32)],
        compiler_params=_params(("arbitrary", "arbitrary", "arbitrary")),
        name="sb_prompt",
    )(q_sb, k_bf, v_bf)


def _page_copy(cache, buf, sem, layer, page, slot, p, a):
    return pltpu.make_async_copy(cache.at[layer, page], buf.at[slot, p], sem.at[slot, a])


def _stream_pages(pt_ref, layer, n_pages, caches, bufs, sem):
    s = pl.program_id(0)
    n_seq = pl.num_programs(0)
    slot = s % 2

    def start(seq, slt):
        def body(p, carry):
            page = pt_ref[seq * n_pages + p]
            for a, (cache, buf) in enumerate(zip(caches, bufs)):
                _page_copy(cache, buf, sem, layer, page, slt, p, a).start()
            return carry
        lax.fori_loop(0, n_pages, body, 0)

    @pl.when(s == 0)
    def _():
        start(0, 0)

    @pl.when(s + 1 < n_seq)
    def _():
        start(s + 1, 1 - slot)

    def wbody(p, carry):
        for a, (cache, buf) in enumerate(zip(caches, bufs)):
            _page_copy(cache, buf, sem, layer, 0, slot, p, a).wait()
        return carry
    lax.fori_loop(0, n_pages, wbody, 0)
    return slot


def _mla_decode_kernel(pt_ref, q_ref, latn_ref, kpen_ref, wv_ref, clat_ref, ckr_ref, o_ref,
                       lat_buf, kr_buf, z_ref, sem, *, layer, n_pages, page, ck):
    slot = _stream_pages(pt_ref, layer, n_pages, [clat_ref, ckr_ref], [lat_buf, kr_buf], sem)
    n_chunks = n_pages * page // ck
    ppc = ck // page

    def chunk(buf, c):
        blk = buf[slot, pl.ds(c * ppc, ppc)]
        return _bf(blk.reshape(ck, blk.shape[-1]))
    q = q_ref[0]
    ql = q[:, 0:KV_LORA]
    qp = q[:, KV_LORA:KV_LORA + MLA_ROPE]
    lat_n = latn_ref[0]
    kpe_n = kpen_ref[0]
    z_new = (jnp.sum(ql.astype(_F32) * _bf(lat_n).astype(_F32), axis=1, keepdims=True)
             + jnp.sum(qp.astype(_F32) * _bf(kpe_n).astype(_F32), axis=1, keepdims=True))

    def score(c, m_run):
        z = _dot_nt(ql, chunk(lat_buf, c)) + _dot_nt(qp, chunk(kr_buf, c))
        z_ref[pl.ds(pl.multiple_of(c * MLA_HEADS, MLA_HEADS), MLA_HEADS), :] = z
        return jnp.maximum(m_run, z)

    m_run = lax.fori_loop(0, n_chunks, score, jnp.full((MLA_HEADS, ck), NEG_INF, _F32))
    m = jnp.maximum(jnp.max(m_run, axis=1, keepdims=True), z_new)

    def attend(c, carry):
        l_run, acc = carry
        p = jnp.exp(z_ref[pl.ds(pl.multiple_of(c * MLA_HEADS, MLA_HEADS), MLA_HEADS), :] - m)
        acc = acc + _dot(_bf(p), chunk(lat_buf, c))
        return l_run + p, acc

    l_run, acc = lax.fori_loop(0, n_chunks, attend,
                               (jnp.zeros((MLA_HEADS, ck), _F32), jnp.zeros((MLA_HEADS, KV_LORA), _F32)))
    p_new = jnp.exp(z_new - m)
    l = jnp.sum(l_run, axis=1, keepdims=True) + p_new
    o_lat = (acc + _bf(p_new).astype(_F32) * _bf(lat_n).astype(_F32)) / jnp.maximum(l, 1e-30)
    for hd in range(MLA_HEADS):
        o_ref[0, :, hd * MLA_V:(hd + 1) * MLA_V] = _bf(_dot(_bf(o_lat[hd:hd + 1]), wv_ref[hd]))


def _mla_decode(pt_flat, layer, q, lat_new, kpe_new, wv, cache_lat, cache_kr):
    s = q.shape[0]
    page = cache_lat.shape[2]
    n_pages = pt_flat.shape[0] // s
    keys = n_pages * page
    ck = min(512, keys)
    grid_spec = pltpu.PrefetchScalarGridSpec(
        num_scalar_prefetch=1,
        grid=(s,),
        in_specs=[pl.BlockSpec((1, MLA_HEADS, Q_KEY_WIDTH), lambda i, pt: (i, 0, 0)),
                  pl.BlockSpec((1, 1, KV_LORA), lambda i, pt: (i, 0, 0)),
                  pl.BlockSpec((1, 1, MLA_ROPE), lambda i, pt: (i, 0, 0)),
                  pl.BlockSpec(wv.shape, lambda i, pt: (0, 0, 0)),
                  pl.BlockSpec(memory_space=pl.ANY),
                  pl.BlockSpec(memory_space=pl.ANY)],
        out_specs=pl.BlockSpec((1, 1, MLA_HEADS * MLA_V), lambda i, pt: (i, 0, 0)),
        scratch_shapes=[pltpu.VMEM((2, n_pages, page, KV_LORA), _F32),
                        pltpu.VMEM((2, n_pages, page, MLA_ROPE), _F32),
                        pltpu.VMEM((keys // ck * MLA_HEADS, ck), _F32),
                        pltpu.SemaphoreType.DMA((2, 2))])
    return pl.pallas_call(
        functools.partial(_mla_decode_kernel, layer=layer, n_pages=n_pages, page=page, ck=ck),
        grid_spec=grid_spec,
        out_shape=jax.ShapeDtypeStruct((s, 1, MLA_HEADS * MLA_V), _BF16),
        compiler_params=_params(("arbitrary",)),
        name="mla_decode",
    )(pt_flat, q, lat_new, kpe_new, wv, cache_lat, cache_kr)


def _sb_decode_kernel(pt_ref, q_ref, ck_ref, cv_ref, o_ref, k_buf, v_buf, sem, *, layer, n_pages, page):
    slot = _stream_pages(pt_ref, layer, n_pages, [ck_ref, cv_ref], [k_buf, v_buf], sem)
    n_chunks = n_pages
    ck = page
    group = SB_HEADS // SB_KV_HEADS
    q = q_ref[0]
    head = lax.broadcasted_iota(jnp.int32, (SB_HEADS, SB_DIM), 0)
    q_by_group = [jnp.where(head // group == g, q, jnp.zeros_like(q)) for g in range(SB_KV_HEADS)]
    head_a = lax.broadcasted_iota(jnp.int32, (SB_HEADS, ck), 0)
    u = _suffix_matrix(ck)

    def body(j, state):
        carry, acc = state
        c = n_chunks - 1 - j
        ks = [k_buf[slot, c, pl.ds(g, ck, stride=SB_KV_HEADS), :] for g in range(SB_KV_HEADS)]
        vs = [v_buf[slot, c, pl.ds(g, ck, stride=SB_KV_HEADS), :] for g in range(SB_KV_HEADS)]
        z = sum(_dot_nt(q_by_group[g], _bf(ks[g])) for g in range(SB_KV_HEADS))
        ls = _log_sigmoid(z)
        lr = ls - z
        a = jnp.exp(ls + _split_dot(lr, u) + carry)
        for g in range(SB_KV_HEADS):
            acc = acc + _dot(_bf(jnp.where(head_a // group == g, a, 0.0)), _bf(vs[g]))
        return carry + jnp.sum(lr, axis=1, keepdims=True), acc

    _, acc = lax.fori_loop(0, n_chunks, body,
                           (jnp.zeros((SB_HEADS, 1), _F32), jnp.zeros((SB_HEADS, SB_DIM), _F32)))
    o_ref[0] = _bf(acc)


def _sb_decode(pt_flat, layer, q, cache_k, cache_v):
    s = q.shape[0]
    rows = cache_k.shape[2]
    page = rows // SB_KV_HEADS
    n_pages = pt_flat.shape[0] // s
    grid_spec = pltpu.PrefetchScalarGridSpec(
        num_scalar_prefetch=1,
        grid=(s,),
        in_specs=[pl.BlockSpec((1, SB_HEADS, SB_DIM), lambda i, pt: (i, 0, 0)),
                  pl.BlockSpec(memory_space=pl.ANY),
                  pl.BlockSpec(memory_space=pl.ANY)],
        out_specs=pl.BlockSpec((1, SB_HEADS, SB_DIM), lambda i, pt: (i, 0, 0)),
        scratch_shapes=[pltpu.VMEM((2, n_pages, rows, SB_DIM), _F32),
                        pltpu.VMEM((2, n_pages, rows, SB_DIM), _F32),
                        pltpu.SemaphoreType.DMA((2, 2))])
    return pl.pallas_call(
        functools.partial(_sb_decode_kernel, layer=layer, n_pages=n_pages, page=page),
        grid_spec=grid_spec,
        out_shape=jax.ShapeDtypeStruct((s, SB_HEADS, SB_DIM), _BF16),
        compiler_params=_params(("arbitrary",)),
        name="sb_decode",
    )(pt_flat, q, cache_k, cache_v)


def _out_proj_kernel(*refs, n_in):
    x_ref, gate_ref = refs[0], refs[1]
    a_refs = refs[2:2 + n_in]
    w_refs = refs[2 + n_in:2 + 2 * n_in]
    o_ref = refs[2 + 2 * n_in]
    y = _dot(a_refs[0][0], w_refs[0][...])
    for a_ref, w_ref in zip(a_refs[1:], w_refs[1:]):
        y = y + _dot(a_ref[0], w_ref[...])
    o_ref[0] = x_ref[0] + gate_ref[0] * y


def _out_proj(x, gate, acts, weights, tm):
    b, t, d = x.shape
    n_in = len(acts)
    in_specs = [pl.BlockSpec((1, tm, d), lambda bi, i: (bi, i, 0)), _mod_spec(gate, tm)]
    in_specs += [pl.BlockSpec((1, tm, a.shape[-1]), lambda bi, i: (bi, i, 0)) for a in acts]
    in_specs += [_full_spec(w) for w in weights]
    return pl.pallas_call(
        functools.partial(_out_proj_kernel, n_in=n_in),
        grid=(b, t // tm),
        in_specs=in_specs,
        out_specs=pl.BlockSpec((1, tm, d), lambda bi, i: (bi, i, 0)),
        out_shape=jax.ShapeDtypeStruct((b, t, d), _F32),
        compiler_params=_params(("arbitrary", "arbitrary")),
        name="out_proj",
    )(x, gate, *acts, *weights)


def _router_kernel(x_ref, sh_ref, sc_ref, w_ref, b_ref, h_ref, gate_ref):
    h = _modnorm(x_ref[0], sh_ref[0], sc_ref[0])
    h_ref[0] = _bf(h)
    logits = jnp.dot(h, w_ref[...], precision=lax.Precision.HIGHEST, preferred_element_type=_F32) + b_ref[...]
    lane = lax.broadcasted_iota(jnp.int32, logits.shape, 1)
    big = jnp.int32(LANES)
    is_g = lane < N_GROUPS
    g_max = jnp.max(jnp.where(is_g, logits, NEG_INF), axis=1, keepdims=True)
    g_idx = jnp.min(jnp.where(is_g & (logits == g_max), lane, big), axis=1, keepdims=True)
    lse = g_max + jnp.log(jnp.sum(jnp.where(is_g, jnp.exp(logits - g_max), 0.0), axis=1, keepdims=True))
    p_group = jnp.exp(g_max - lse)
    in_grp = ((lane >= GATE_LANE0) & (lane < GATE_LANE0 + N_EXPERTS)
              & ((lane - GATE_LANE0) // EXPERTS_PER_GROUP == g_idx))
    e1 = jnp.max(jnp.where(in_grp, logits, NEG_INF), axis=1, keepdims=True)
    i1 = jnp.min(jnp.where(in_grp & (logits == e1), lane, big), axis=1, keepdims=True)
    rest = in_grp & (lane != i1)
    e2 = jnp.max(jnp.where(rest, logits, NEG_INF), axis=1, keepdims=True)
    i2 = jnp.min(jnp.where(rest & (logits == e2), lane, big), axis=1, keepdims=True)
    t2 = jnp.exp(e2 - e1)
    w1 = 1.0 / (1.0 + t2)
    w2 = t2 / (1.0 + t2)
    gate_ref[0] = jnp.where(lane == i1, w1 * p_group, jnp.where(lane == i2, w2 * p_group, 0.0))


def _router(x, shift, scale, w_rt, b_rt, tm):
    b, t, d = x.shape
    return pl.pallas_call(
        _router_kernel,
        grid=(b, t // tm),
        in_specs=[pl.BlockSpec((1, tm, d), lambda bi, i: (bi, i, 0)), _mod_spec(shift, tm), _mod_spec(scale, tm),
                  _full_spec(w_rt), _full_spec(b_rt)],
        out_specs=[pl.BlockSpec((1, tm, d), lambda bi, i: (bi, i, 0)),
                   pl.BlockSpec((1, tm, LANES), lambda bi, i: (bi, i, 0))],
        out_shape=[jax.ShapeDtypeStruct((b, t, d), _BF16), jax.ShapeDtypeStruct((b, t, LANES), _F32)],
        compiler_params=_params(("arbitrary", "arbitrary")),
        name="moe_router",
    )(x, shift, scale, w_rt, b_rt)


def _experts_kernel(x_ref, g2_ref, h_ref, gate_ref, wg_ref, wu_ref, wd_ref, o_ref, acc_ref):
    e = pl.program_id(2)

    @pl.when(e == 0)
    def _():
        acc_ref[...] = jnp.zeros(acc_ref.shape, _F32)

    h = h_ref[0]
    gate = gate_ref[0]
    lane = lax.broadcasted_iota(jnp.int32, gate.shape, 1)
    g_col = jnp.sum(jnp.where(lane == GATE_LANE0 + e, gate, 0.0), axis=1, keepdims=True)
    a = _dot(h, wg_ref[0])
    hid = a * _sigmoid(a) * _dot(h, wu_ref[0]) * g_col
    acc_ref[...] += _dot(_bf(hid), wd_ref[0])

    @pl.when(e == pl.num_programs(2) - 1)
    def _():
        o_ref[0] = x_ref[0] + g2_ref[0] * acc_ref[...]


def _experts(x, gate2, h_bf, gate, wg, wu, wd, tm):
    b, t, d = x.shape
    n_exp, _, hidden = wg.shape
    mod3 = (pl.BlockSpec((1, 1, d), lambda bi, i, e: (bi, 0, 0)) if gate2.shape[1] == 1
            else pl.BlockSpec((1, tm, d), lambda bi, i, e: (bi, i, 0)))
    return pl.pallas_call(
        _experts_kernel,
        grid=(b, t // tm, n_exp),
        in_specs=[pl.BlockSpec((1, tm, d), lambda bi, i, e: (bi, i, 0)), mod3,
                  pl.BlockSpec((1, tm, d), lambda bi, i, e: (bi, i, 0)),
                  pl.BlockSpec((1, tm, LANES), lambda bi, i, e: (bi, i, 0)),
                  pl.BlockSpec((1, d, hidden), lambda bi, i, e: (e, 0, 0)),
                  pl.BlockSpec((1, d, hidden), lambda bi, i, e: (e, 0, 0)),
                  pl.BlockSpec((1, hidden, d), lambda bi, i, e: (e, 0, 0))],
        out_specs=pl.BlockSpec((1, tm, d), lambda bi, i, e: (bi, i, 0)),
        out_shape=jax.ShapeDtypeStruct((b, t, d), _F32),
        scratch_shapes=[pltpu.VMEM((tm, d), _F32)],
        compiler_params=_params(("arbitrary", "arbitrary", "arbitrary")),
        name="moe_experts",
    )(x, gate2, h_bf, gate, wg, wu, wd)


def _odd_in_kernel(x_ref, sh_ref, sc_ref, w_ref, q_ref, gates_ref,
                 ck_ref, cv_ref, sk_ref, sv_ref, wk_ref, wv_ref, skb_ref, svb_ref, wkb_ref, wvb_ref):
    h = _modnorm(x_ref[0], sh_ref[0], sc_ref[0])
    y = _dot(_bf(h), w_ref[...])
    for hd in range(NSA_HEADS):
        q_ref[0, hd] = _bf(y[:, hd * NSA_DIM:(hd + 1) * NSA_DIM] * NSA_SCALE)
    base = NSA_HEADS * NSA_DIM
    parts = [y[:, base + j * NSA_DIM:base + (j + 1) * NSA_DIM] for j in range(7)]
    for ref, part in zip((ck_ref, cv_ref, sk_ref, sv_ref, wk_ref, wv_ref), parts[:6]):
        ref[0] = part
    for ref, part in zip((skb_ref, svb_ref, wkb_ref, wvb_ref), parts[2:6]):
        ref[0] = _bf(part)
    gates_ref[0] = _sigmoid(parts[6])


def _odd_in(x, shift, scale, w_ext, tm):
    b, t, d = x.shape
    row = lambda n: pl.BlockSpec((1, tm, n), lambda bi, i: (bi, i, 0))
    sds = jax.ShapeDtypeStruct
    return pl.pallas_call(
        _odd_in_kernel,
        grid=(b, t // tm),
        in_specs=[row(d), _mod_spec(shift, tm), _mod_spec(scale, tm), _full_spec(w_ext)],
        out_specs=[pl.BlockSpec((1, NSA_HEADS, tm, NSA_DIM), lambda bi, i: (bi, 0, i, 0)), row(LANES)]
        + [row(NSA_DIM)] * 10,
        out_shape=[sds((b, NSA_HEADS, t, NSA_DIM), _BF16), sds((b, t, LANES), _F32)]
        + [sds((b, t, NSA_DIM), _F32)] * 6 + [sds((b, t, NSA_DIM), _BF16)] * 4,
        compiler_params=_params(("arbitrary", "arbitrary")),
        name="odd_in",
    )(x, shift, scale, w_ext)


def _block_mean_kernel(k_ref, v_ref, ok_ref, ov_ref):
    nb = ok_ref.shape[1]
    ok_ref[0] = jnp.sum(k_ref[0].reshape(nb, NSA_BLOCK, NSA_DIM), axis=1) * (1.0 / NSA_BLOCK)
    ov_ref[0] = jnp.sum(v_ref[0].reshape(nb, NSA_BLOCK, NSA_DIM), axis=1) * (1.0 / NSA_BLOCK)


def _block_means(ck, cv):
    b, t, _ = ck.shape
    nb = t // NSA_BLOCK
    spec_in = pl.BlockSpec((1, t, NSA_DIM), lambda bi: (bi, 0, 0))
    spec_out = pl.BlockSpec((1, nb, NSA_DIM), lambda bi: (bi, 0, 0))
    return pl.pallas_call(
        _block_mean_kernel,
        grid=(b,),
        in_specs=[spec_in, spec_in],
        out_specs=[spec_out, spec_out],
        out_shape=[jax.ShapeDtypeStruct((b, nb, NSA_DIM), _F32)] * 2,
        compiler_params=_params(("arbitrary",)),
        name="nsa_block_means",
    )(ck, cv)


def _alibi_slopes_col(rows, rows_per_head):
    head = lax.broadcasted_iota(jnp.int32, (rows, 1), 0) // rows_per_head
    return jnp.exp2(-8.0 * (head + 1).astype(_F32) / NSA_HEADS)


def _topk_select(imp, n_slots):
    n, nb = imp.shape
    blk = lax.broadcasted_iota(jnp.int32, (n, nb), 1)
    rank = jnp.zeros((n, nb), _F32)
    for j in range(nb):
        col = imp[:, j:j + 1]
        ahead = (col > imp) | ((col == imp) & (blk > j))
        rank = rank + jnp.where(ahead, 1.0, 0.0)
    return rank < n_slots


def _nsa_prompt_kernel(q_ref, gates_ref, bmk_ref, bmv_ref, sk_ref, sv_ref, wk_ref, wv_ref, exp_ref, o_ref,
                       m_ref, l_ref, acc_ref, *, tq, tk):
    i = pl.program_id(1)
    nb = bmk_ref.shape[1]
    rows = NSA_HEADS * tq
    q = q_ref[0].reshape(rows, NSA_DIM)
    slope = _alibi_slopes_col(rows, tq)
    slope3 = slope.reshape(NSA_HEADS, tq, 1)
    tpos = i * tq + lax.broadcasted_iota(jnp.int32, (tq, 1), 0)

    blk = lax.broadcasted_iota(jnp.int32, (tq, nb), 1)
    cdist = tpos - ((blk + 1) * NSA_BLOCK - 1)
    cmask = cdist >= 0
    s = _dot_nt(q, _bf(bmk_ref[0])).reshape(NSA_HEADS, tq, nb) - slope3 * cdist.astype(_F32)[None]
    s = jnp.where(cmask[None], s, NEG_INF)
    m = jnp.max(s, axis=2, keepdims=True)
    p = jnp.where(cmask[None], jnp.exp(s - m), 0.0)
    p_cmp = p / jnp.maximum(jnp.sum(p, axis=2, keepdims=True), 1e-30)
    o_cmp = _dot(_bf(p_cmp.reshape(rows, nb)), _bf(bmv_ref[0]))

    qblk = tpos // NSA_BLOCK
    forced = (blk == 0) | (blk == qblk) | (blk == qblk - 1)
    imp = jnp.where(forced, NSA_FORCE, jnp.sum(p_cmp, axis=0))
    imp = jnp.where(blk <= qblk, imp, -1.0)
    sel = _topk_select(imp, min(NSA_TOPK, nb)) & (imp >= 0.0)
    sel_bf = jnp.where(sel, 1.0, 0.0).astype(_BF16)

    kcol = lax.broadcasted_iota(jnp.int32, (tq, tk), 1)

    def attend(k_ref, v_ref, c_lo, c_hi, mask_fn):
        m_ref[...] = jnp.full((rows, 1), NEG_INF, _F32)
        l_ref[...] = jnp.zeros((rows, 1), _F32)
        acc_ref[...] = jnp.zeros((rows, NSA_DIM), _F32)

        def body(c, carry):
            start = pl.multiple_of(c * tk, tk)
            k = k_ref[0, pl.ds(start, tk), :]
            v = v_ref[0, pl.ds(start, tk), :]
            dist = tpos - (c * tk + kcol)
            mask = mask_fn(start, dist)
            s3 = _dot_nt(q, k).reshape(NSA_HEADS, tq, tk) - slope3 * dist.astype(_F32)[None]
            s3 = jnp.where(mask[None], s3, NEG_INF)
            m_old = m_ref[...].reshape(NSA_HEADS, tq, 1)
            m_new = jnp.maximum(m_old, jnp.max(s3, axis=2, keepdims=True))
            alpha = jnp.exp(m_old - m_new)
            p3 = jnp.where(mask[None], jnp.exp(s3 - m_new), 0.0)
            l_ref[...] = (alpha * l_ref[...].reshape(NSA_HEADS, tq, 1)
                          + jnp.sum(p3, axis=2, keepdims=True)).reshape(rows, 1)
            acc_ref[...] = (alpha.reshape(rows, 1) * acc_ref[...]
                            + _dot(_bf(p3.reshape(rows, tk)), v))
            m_ref[...] = m_new.reshape(rows, 1)
            return carry

        lax.fori_loop(c_lo, c_hi, body, 0)
        return acc_ref[...] / jnp.maximum(l_ref[...], 1e-30)

    def sel_mask(start, dist):
        chosen = _dot(sel_bf, exp_ref[:, pl.ds(start, tk)])
        return (chosen > 0.5) & (dist >= 0)

    def win_mask(start, dist):
        return (dist >= 0) & (dist < NSA_WINDOW)

    n_diag = (i * tq + tq + tk - 1) // tk
    o_sel = attend(sk_ref, sv_ref, 0, n_diag, sel_mask)
    w_lo = jnp.maximum(i * tq - NSA_WINDOW, 0) // tk
    o_win = attend(wk_ref, wv_ref, w_lo, n_diag, win_mask)

    gates = gates_ref[0]
    for hd in range(NSA_HEADS):
        sl = slice(hd * tq, (hd + 1) * tq)
        o = (gates[:, 3 * hd:3 * hd + 1] * o_cmp[sl] + gates[:, 3 * hd + 1:3 * hd + 2] * o_sel[sl]
             + gates[:, 3 * hd + 2:3 * hd + 3] * o_win[sl])
        o_ref[0, :, hd * NSA_DIM:(hd + 1) * NSA_DIM] = _bf(o)


def _nsa_prompt(q, gates, bmk, bmv, sk, sv, wk, wv, expand):
    b, _, t, _ = q.shape
    tq = 128
    tk = min(256, t)
    nb = bmk.shape[1]
    rows = NSA_HEADS * tq
    full = lambda n: pl.BlockSpec((1, n, NSA_DIM), lambda bi, i: (bi, 0, 0))
    return pl.pallas_call(
        functools.partial(_nsa_prompt_kernel, tq=tq, tk=tk),
        grid=(b, t // tq),
        in_specs=[pl.BlockSpec((1, NSA_HEADS, tq, NSA_DIM), lambda bi, i: (bi, 0, i, 0)),
                  pl.BlockSpec((1, tq, LANES), lambda bi, i: (bi, i, 0)),
                  full(nb), full(nb), full(t), full(t), full(t), full(t), _full_spec(expand)],
        out_specs=pl.BlockSpec((1, tq, NSA_HEADS * NSA_DIM), lambda bi, i: (bi, i, 0)),
        out_shape=jax.ShapeDtypeStruct((b, t, NSA_HEADS * NSA_DIM), _BF16),
        scratch_shapes=[pltpu.VMEM((rows, 1), _F32), pltpu.VMEM((rows, 1), _F32),
                        pltpu.VMEM((rows, NSA_DIM), _F32)],
        compiler_params=_params(("arbitrary", "arbitrary")),
        name="nsa_prompt",
    )(q, gates, bmk, bmv, sk, sv, wk, wv, expand)


def _nsa_decode_kernel(pt_ref, q_ref, gates_ref, new_ref, wink_ref, winv_ref, exp_ref,
                       cck_ref, ccv_ref, csk_ref, csv_ref, o_ref,
                       ck_buf, cv_buf, sk_buf, sv_buf, z_ref, sem, *, layer, n_pages, page, ck):
    slot = _stream_pages(pt_ref, layer, n_pages, [cck_ref, ccv_ref, csk_ref, csv_ref],
                         [ck_buf, cv_buf, sk_buf, sv_buf], sem)
    keys = n_pages * page
    nbp = keys // NSA_BLOCK
    n_chunks = keys // ck
    ppc = ck // page
    hds = NSA_HEADS

    def chunk(buf, c):
        return _bf(buf[slot, pl.ds(c * ppc, ppc)].reshape(ck, NSA_DIM))
    q = q_ref[0]
    qf = q.astype(_F32)
    slope = _alibi_slopes_col(hds, 1)
    new = new_ref[0]
    sk_n, sv_n, wk_n, wv_n = (_bf(new[j:j + 1]).astype(_F32) for j in range(4))

    bmk = jnp.sum(ck_buf[slot].reshape(nbp, NSA_BLOCK, NSA_DIM), axis=1) * (1.0 / NSA_BLOCK)
    bmv = jnp.sum(cv_buf[slot].reshape(nbp, NSA_BLOCK, NSA_DIM), axis=1) * (1.0 / NSA_BLOCK)
    blk = lax.broadcasted_iota(jnp.int32, (1, nbp), 1)
    cdist = (keys - ((blk + 1) * NSA_BLOCK - 1)).astype(_F32)
    s = _dot_nt(q, _bf(bmk)) - slope * cdist
    p = jnp.exp(s - jnp.max(s, axis=1, keepdims=True))
    p_cmp = p / jnp.maximum(jnp.sum(p, axis=1, keepdims=True), 1e-30)
    o_cmp = _dot(_bf(p_cmp), _bf(bmv))

    qblk = keys // NSA_BLOCK
    forced = (blk == 0) | (blk == qblk - 1)
    imp = jnp.where(forced, NSA_FORCE, jnp.sum(p_cmp, axis=0, keepdims=True))
    imp_b = jnp.broadcast_to(imp, (nbp, nbp))
    imp_t = imp_b.T
    jrow = lax.broadcasted_iota(jnp.int32, (nbp, nbp), 0)
    bcol = lax.broadcasted_iota(jnp.int32, (nbp, nbp), 1)
    ahead = (imp_t > imp_b) | ((imp_t == imp_b) & (jrow < bcol))
    rank = jnp.sum(jnp.where(ahead, 1.0, 0.0), axis=0, keepdims=True)
    sel = jnp.where(rank < min(NSA_TOPK, nbp + 1) - 1, 1.0, 0.0).astype(_BF16)

    kcol = lax.broadcasted_iota(jnp.int32, (1, ck), 1)

    def score(c, m_run):
        start = pl.multiple_of(c * ck, ck)
        chosen = _dot(sel, exp_ref[:, pl.ds(start, ck)])
        dist = (keys - (c * ck + kcol)).astype(_F32)
        z = _dot_nt(q, chunk(sk_buf, c)) - slope * dist
        z = jnp.where(chosen > 0.5, z, NEG_INF)
        z_ref[pl.ds(pl.multiple_of(c * hds, hds), hds), :] = z
        return jnp.maximum(m_run, z)

    m_run = lax.fori_loop(0, n_chunks, score, jnp.full((hds, ck), NEG_INF, _F32))
    z_new = jnp.sum(qf * sk_n, axis=1, keepdims=True)
    m = jnp.maximum(jnp.max(m_run, axis=1, keepdims=True), z_new)

    def attend(c, carry):
        l_run, acc = carry
        z = z_ref[pl.ds(pl.multiple_of(c * hds, hds), hds), :]
        p_c = jnp.where(z > 0.5 * NEG_INF, jnp.exp(z - m), 0.0)
        return l_run + p_c, acc + _dot(_bf(p_c), chunk(sv_buf, c))

    l_run, acc = lax.fori_loop(0, n_chunks, attend,
                               (jnp.zeros((hds, ck), _F32), jnp.zeros((hds, NSA_DIM), _F32)))
    p_new = jnp.exp(z_new - m)
    l = jnp.sum(l_run, axis=1, keepdims=True) + p_new
    o_sel = (acc + _bf(p_new).astype(_F32) * sv_n) / jnp.maximum(l, 1e-30)

    n_buf = wink_ref.shape[2]
    wdist = n_buf - lax.broadcasted_iota(jnp.int32, (1, n_buf), 1)
    wmask = wdist < NSA_WINDOW
    zw = _dot_nt(q, _bf(wink_ref[0, 0])) - slope * wdist.astype(_F32)
    zw = jnp.where(wmask, zw, NEG_INF)
    zw_new = jnp.sum(qf * wk_n, axis=1, keepdims=True)
    mw = jnp.maximum(jnp.max(zw, axis=1, keepdims=True), zw_new)
    pw = jnp.where(wmask, jnp.exp(zw - mw), 0.0)
    pw_new = jnp.exp(zw_new - mw)
    lw = jnp.sum(pw, axis=1, keepdims=True) + pw_new
    o_win = (_dot(_bf(pw), _bf(winv_ref[0, 0])) + _bf(pw_new).astype(_F32) * wv_n) / jnp.maximum(lw, 1e-30)

    gates = gates_ref[0]
    o_ref[0] = _bf(gates[:, 0:1] * o_cmp + gates[:, 1:2] * o_sel + gates[:, 2:3] * o_win)


def _nsa_decode(pt_flat, layer, q, gates, new_rows, win_k, win_v, expand, cck, ccv, csk, csv):
    s = q.shape[0]
    page = cck.shape[2]
    n_pages = pt_flat.shape[0] // s
    keys = n_pages * page
    ck = min(512, keys)
    n_buf = win_k.shape[2]
    any_spec = pl.BlockSpec(memory_space=pl.ANY)
    grid_spec = pltpu.PrefetchScalarGridSpec(
        num_scalar_prefetch=1,
        grid=(s,),
        in_specs=[pl.BlockSpec((1, NSA_HEADS, NSA_DIM), lambda i, pt: (i, 0, 0)),
                  pl.BlockSpec((1, NSA_HEADS, LANES), lambda i, pt: (i, 0, 0)),
                  pl.BlockSpec((1, 8, NSA_DIM), lambda i, pt: (i, 0, 0)),
                  pl.BlockSpec((1, 1, n_buf, NSA_DIM), lambda i, pt: (layer, i, 0, 0)),
                  pl.BlockSpec((1, 1, n_buf, NSA_DIM), lambda i, pt: (layer, i, 0, 0)),
                  pl.BlockSpec(expand.shape, lambda i, pt: (0, 0)),
                  any_spec, any_spec, any_spec, any_spec],
        out_specs=pl.BlockSpec((1, NSA_HEADS, NSA_DIM), lambda i, pt: (i, 0, 0)),
        scratch_shapes=[pltpu.VMEM((2, n_pages, page, NSA_DIM), _F32)] * 4
        + [pltpu.VMEM((keys // ck * NSA_HEADS, ck), _F32), pltpu.SemaphoreType.DMA((2, 4))])
    return pl.pallas_call(
        functools.partial(_nsa_decode_kernel, layer=layer, n_pages=n_pages, page=page, ck=ck),
        grid_spec=grid_spec,
        out_shape=jax.ShapeDtypeStruct((s, NSA_HEADS, NSA_DIM), _BF16),
        compiler_params=_params(("arbitrary",)),
        name="nsa_decode",
    )(pt_flat, q, gates, new_rows, win_k, win_v, expand, cck, ccv, csk, csv)


def _final_norm_kernel(x_ref, g_ref, o_ref):
    x = x_ref[0]
    o_ref[0] = x * lax.rsqrt(jnp.mean(x * x, axis=-1, keepdims=True) + NORM_EPS) * g_ref[...]


def _final_norm(x, g, tm):
    b, t, d = x.shape
    return pl.pallas_call(
        _final_norm_kernel,
        grid=(b, t // tm),
        in_specs=[pl.BlockSpec((1, tm, d), lambda bi, i: (bi, i, 0)), _full_spec(g)],
        out_specs=pl.BlockSpec((1, tm, d), lambda bi, i: (bi, i, 0)),
        out_shape=jax.ShapeDtypeStruct((b, t, d), _F32),
        compiler_params=_params(("arbitrary", "arbitrary")),
        name="final_norm",
    )(x, g)


def _rope_tables(pos):
    half = MLA_ROPE // 2
    inv_freq = ROPE_THETA ** (-jnp.arange(half, dtype=_F32) / half)
    ang = pos.astype(_F32)[:, None] * inv_freq[None, :]
    cos, sin = jnp.cos(ang), jnp.sin(ang)
    pad = jnp.zeros((pos.shape[0], LANES - MLA_ROPE), _F32)
    return jnp.concatenate([cos, cos, pad], axis=1), jnp.concatenate([-sin, sin, pad], axis=1)


def _swap_halves(w):
    half = w.shape[-1] // 2
    return jnp.concatenate([w[..., half:], w[..., :half]], axis=-1)


def _pad_lanes(w, n):
    return jnp.pad(w, [(0, 0)] * (w.ndim - 1) + [(0, n - w.shape[-1])])


def _even_weights(w_in, w_q_up, w_kv_up, w_out):
    d = w_in.shape[0]
    kpe = w_in[:, 768:832]
    w_ext = jnp.concatenate([w_in[:, :768], w_in[:, 832:2368],
                             _pad_lanes(kpe, LANES), _pad_lanes(_swap_halves(kpe), LANES)], axis=1)
    rope = w_q_up[:, :, MLA_NOPE:]
    wq_all = jnp.concatenate([w_q_up[:, :, :MLA_NOPE].reshape(Q_LORA, -1),
                              _pad_lanes(rope, LANES).reshape(Q_LORA, -1),
                              _pad_lanes(_swap_halves(rope), LANES).reshape(Q_LORA, -1)], axis=1)
    wk_t = jnp.transpose(w_kv_up[:, :, :MLA_NOPE], (1, 2, 0))
    wv = jnp.transpose(w_kv_up[:, :, MLA_NOPE:], (1, 0, 2))
    split = MLA_HEADS * MLA_V
    assert w_ext.shape == (d, EVEN_COLS)
    return _bf(w_ext), _bf(wq_all), _bf(wk_t), _bf(wv), _bf(w_out[:split]), _bf(w_out[split:])


def _expansion(n_blocks):
    r = jnp.arange(n_blocks)[:, None]
    c = jnp.arange(n_blocks * NSA_BLOCK)[None, :] // NSA_BLOCK
    return (r == c).astype(_BF16)


def kernel(x_prompt, x_sample, cache_mla_latent, cache_mla_krope, cache_sb_k, cache_sb_v, cache_nsa_cmp_k, cache_nsa_cmp_v, cache_nsa_sel_k, cache_nsa_sel_v, state_nsa_win_k, state_nsa_win_v, page_table, c_prompt, c_sample, w_mod, b_mod, w_even_in, g_q_norm, w_q_up, g_kv_norm, w_kv_up, w_even_out, w_odd_in, w_odd_out, w_group, b_group, w_router, b_router, w_exp_gate, w_exp_up, w_exp_down, g_final):
    bp, t, d = x_prompt.shape
    s = x_sample.shape[0]
    assert x_sample.shape[1] == 1
    depth = w_mod.shape[0]
    n_pages = page_table.shape[1]
    page = cache_mla_latent.shape[2]
    past_len = n_pages * page
    pool = cache_mla_latent.shape[1]
    tm_p = min(256, t)
    tm_big = min(512, t)

    pt_flat = page_table.reshape(-1).astype(jnp.int32)
    cos_p, sin_p = _rope_tables(jnp.arange(t))
    cos_s, sin_s = _rope_tables(jnp.full((s,), past_len))
    expand_p = _expansion(t // NSA_BLOCK)
    expand_s = _expansion(past_len // NSA_BLOCK)

    mc = bp + s
    mc_pad = -(-mc // 8) * 8
    c_all = jnp.pad(jnp.concatenate([c_prompt, c_sample], axis=0), ((0, mc_pad - mc), (0, 0)))
    mods = _adaln(c_all, w_mod, b_mod).reshape(depth, mc_pad, 6, d)

    xp = x_prompt
    xs = x_sample.reshape(1, s, d)
    sb_k_view = cache_sb_k.reshape(cache_sb_k.shape[0], pool, page * SB_KV_HEADS, SB_DIM)
    sb_v_view = cache_sb_v.reshape(cache_sb_v.shape[0], pool, page * SB_KV_HEADS, SB_DIM)
    even_p, even_s = [[] for _ in range(4)], [[] for _ in range(4)]
    odd_p, odd_s = [[] for _ in range(6)], [[] for _ in range(6)]

    for l in range(depth):
        mp = [mods[l, :bp, j].reshape(bp, 1, d) for j in range(6)]
        ms = [mods[l, bp:mc, j].reshape(1, s, d) for j in range(6)]
        i = l // 2
        if l % 2 == 0:
            w_ext, wq_all, wk_t, wv, w_out_a, w_out_b = _even_weights(w_even_in[i], w_q_up[i], w_kv_up[i], w_even_out[i])
            gq = g_q_norm[i].reshape(1, -1)
            gkv = g_kv_norm[i].reshape(1, -1)
            lat, kpe, ksb, vsb, kfull, ksbb, vsbb, qf, qsb = _even_in(
                xp, mp[0], mp[1], cos_p, sin_p, w_ext, gq, gkv, wq_all, wk_t, tm_p)
            o_mla = _mla_prompt(qf, kfull, wv)
            o_sb = _sb_prompt(qsb, ksbb, vsbb)
            xp = _out_proj(xp, mp[2], [o_mla, o_sb], [w_out_a, w_out_b], tm_big)
            for lst, a in zip(even_p, (lat, kpe, ksb.reshape(bp, t, SB_KV_HEADS, SB_DIM),
                                       vsb.reshape(bp, t, SB_KV_HEADS, SB_DIM))):
                lst.append(a)

            lat, kpe, ksb, vsb, _, _, _, qf, qsb = _even_in(
                xs, ms[0], ms[1], cos_s, sin_s, w_ext, gq, gkv, wq_all, wk_t, s)
            o_mla = _mla_decode(pt_flat, i, jnp.transpose(qf[0], (1, 0, 2)), lat.reshape(s, 1, KV_LORA),
                                kpe.reshape(s, 1, MLA_ROPE), wv, cache_mla_latent, cache_mla_krope)
            o_sb = _sb_decode(pt_flat, i, jnp.transpose(qsb[0], (1, 0, 2)), sb_k_view, sb_v_view)
            xs = _out_proj(xs, ms[2], [o_mla.reshape(1, s, -1), o_sb.reshape(1, s, -1)], [w_out_a, w_out_b], s)
            for lst, a in zip(even_s, (lat.reshape(s, 1, KV_LORA), kpe.reshape(s, 1, MLA_ROPE),
                                       ksb.reshape(s, 1, SB_KV_HEADS, SB_DIM),
                                       vsb.reshape(s, 1, SB_KV_HEADS, SB_DIM))):
                lst.append(a)
        else:
            w_ext = _bf(_pad_lanes(w_odd_in[i], ODD_COLS))
            w_out = _bf(w_odd_out[i])
            q, gates, ck, cv, sk, sv, wk, wv_, skb, svb, wkb, wvb = _odd_in(xp, mp[0], mp[1], w_ext, tm_p)
            bmk, bmv = _block_means(ck, cv)
            o = _nsa_prompt(q, gates, bmk, bmv, skb, svb, wkb, wvb, expand_p)
            xp = _out_proj(xp, mp[2], [o], [w_out], tm_big)
            keep = min(NSA_WINDOW, t)
            for lst, a in zip(odd_p, (ck, cv, sk, sv, wk[:, t - keep:], wv_[:, t - keep:])):
                lst.append(a)

            q, gates, ck, cv, sk, sv, wk, wv_, _, _, _, _ = _odd_in(xs, ms[0], ms[1], w_ext, s)
            gates_s = _pad_lanes(gates[0, :, :3 * NSA_HEADS].reshape(s, NSA_HEADS, 3), LANES)
            new_rows = jnp.concatenate([sk[0][:, None], sv[0][:, None], wk[0][:, None], wv_[0][:, None],
                                        jnp.zeros((s, 4, NSA_DIM), _F32)], axis=1)
            o = _nsa_decode(pt_flat, i, jnp.transpose(q[0], (1, 0, 2)), gates_s, new_rows,
                            state_nsa_win_k, state_nsa_win_v, expand_s,
                            cache_nsa_cmp_k, cache_nsa_cmp_v, cache_nsa_sel_k, cache_nsa_sel_v)
            xs = _out_proj(xs, ms[2], [o.reshape(1, s, -1)], [w_out], s)
            win_k_new = jnp.concatenate([state_nsa_win_k[i][:, 1:], wk[0][:, None]], axis=1)
            win_v_new = jnp.concatenate([state_nsa_win_v[i][:, 1:], wv_[0][:, None]], axis=1)
            for lst, a in zip(odd_s, (ck.reshape(s, 1, -1), cv.reshape(s, 1, -1), sk.reshape(s, 1, -1),
                                      sv.reshape(s, 1, -1), win_k_new, win_v_new)):
                lst.append(a)

        w_rt = _pad_lanes(jnp.concatenate([w_group[l], w_router[l]], axis=1), LANES)
        b_rt = _pad_lanes(jnp.concatenate([b_group[l], b_router[l]], axis=0).reshape(1, -1), LANES)
        wg, wu, wd = _bf(w_exp_gate[l]), _bf(w_exp_up[l]), _bf(w_exp_down[l])
        h_bf, gate = _router(xp, mp[3], mp[4], w_rt, b_rt, tm_big)
        xp = _experts(xp, mp[5], h_bf, gate, wg, wu, wd, tm_big)
        h_bf, gate = _router(xs, ms[3], ms[4], w_rt, b_rt, s)
        xs = _experts(xs, ms[5], h_bf, gate, wg, wu, wd, s)

    g = g_final.reshape(1, d)
    y_prompt = _final_norm(xp, g, tm_big)
    y_sample = _final_norm(xs, g, s).reshape(s, 1, d)
    lat_p, kr_p, sbk_p, sbv_p = [jnp.stack(a) for a in even_p]
    lat_s, kr_s, sbk_s, sbv_s = [jnp.stack(a) for a in even_s]
    ck_p, cv_p, sk_p, sv_p, wk_p, wv_p = [jnp.stack(a) for a in odd_p]
    ck_s, cv_s, sk_s, sv_s, wk_s, wv_s = [jnp.stack(a) for a in odd_s]
    return (y_prompt, y_sample, lat_p, lat_s, kr_p, kr_s, sbk_p, sbk_s, sbv_p, sbv_s,
            ck_p, ck_s, cv_p, cv_s, sk_p, sk_s, sv_p, sv_s, wk_p, wk_s, wv_p, wv_s)
```
